```python
import math
import jax, jax.numpy as jnp
from jax import lax
import numpy as np

D_MODEL = 1024
BATCH = 4
SEQ = 8192
DEPTH = 1

GRID_W = 64
CTX_LEN = 256
N_HEADS = 16
N_KV_HEADS = 4
HEAD_DIM = D_MODEL // N_HEADS
Q_GROUP = N_HEADS // N_KV_HEADS
WINDOW = 128
BLOCK = 128
ROPE_THETA = 10000.0
N_POOL_GROUPS = 4
POOL_WINDOWS = (2, 4, 8, 16)
POOL_WIDTH = D_MODEL // 2
POOL_GROUP_DIM = POOL_WIDTH // N_POOL_GROUPS
Q_WIDTH = N_HEADS * HEAD_DIM
KV_WIDTH = N_KV_HEADS * HEAD_DIM
N_BRANCHES = 2
GATE_WIDTH = N_BRANCHES * D_MODEL
IN_WIDTH = Q_WIDTH + 2 * KV_WIDTH + POOL_WIDTH + GATE_WIDTH
D_FF = 4 * D_MODEL
N_MOD = 6
ALPHA = (2.0 * DEPTH) ** 0.25
BETA = (8.0 * DEPTH) ** -0.25
LN_EPS = 1e-6
NEG_INF = -1e30

kernel_name = "hybrid_pool_swa_dit_block"


def _layernorm(z, g=None, b=None):
    zf = z.astype(jnp.float32)
    mu = jnp.mean(zf, axis=-1, keepdims=True)
    var = jnp.mean(jnp.square(zf - mu), axis=-1, keepdims=True)
    y = (zf - mu) * lax.rsqrt(var + LN_EPS)
    if g is not None:
        y = y * g.astype(jnp.float32) + b.astype(jnp.float32)
    return y.astype(z.dtype)


def _modulate(z, shift, scale):
    return _layernorm(z) * (1.0 + scale) + shift


def _split_proj(proj):
    B, L, _ = proj.shape
    o = np.cumsum([Q_WIDTH, KV_WIDTH, KV_WIDTH, POOL_WIDTH])
    q = proj[..., :o[0]].reshape(B, L, N_HEADS, HEAD_DIM)
    k = proj[..., o[0]:o[1]].reshape(B, L, N_KV_HEADS, HEAD_DIM)
    v = proj[..., o[1]:o[2]].reshape(B, L, N_KV_HEADS, HEAD_DIM)
    p = proj[..., o[2]:o[3]]
    g = proj[..., o[3]:]
    return q, k, v, p, g


def _axial_rope(z, rows, cols):
    half = HEAD_DIM // 2
    quarter = half // 2
    inv = 1.0 / (ROPE_THETA ** (jnp.arange(quarter, dtype=jnp.float32) / quarter))

    def rot(zh, pos):
        ang = pos.astype(jnp.float32)[:, None] * inv[None, :]
        cos = jnp.cos(ang)[None, :, None, :].astype(zh.dtype)
        sin = jnp.sin(ang)[None, :, None, :].astype(zh.dtype)
        z1, z2 = zh[..., :quarter], zh[..., quarter:]
        return jnp.concatenate([z1 * cos - z2 * sin, z2 * cos + z1 * sin], axis=-1)

    return jnp.concatenate([rot(z[..., :half], rows), rot(z[..., half:], cols)], axis=-1)


def _softmax_with_sink(logits, sink):
    s = jnp.broadcast_to(sink.astype(jnp.float32)[None, :, :, None, None], logits.shape[:-1] + (1,))
    probs = jax.nn.softmax(jnp.concatenate([logits, s], axis=-1), axis=-1)
    return probs[..., :-1]


def _context_attention(q_c, k_c, v_c, sink):
    B, C = q_c.shape[:2]
    qg = q_c.reshape(B, C, N_KV_HEADS, Q_GROUP, HEAD_DIM)
    logits = jnp.einsum('bqkgd,bckd->bkgqc', qg, k_c).astype(jnp.float32) / math.sqrt(HEAD_DIM)
    probs = _softmax_with_sink(logits, sink).astype(v_c.dtype)
    out = jnp.einsum('bkgqc,bckd->bqkgd', probs, v_c)
    return out.reshape(B, C, Q_WIDTH)


def _banded_attention(q, k, v, k_c, v_c, sink):
    B, S = q.shape[:2]
    nb = S // BLOCK
    pad = ((0, 0), (BLOCK, BLOCK), (0, 0), (0, 0))
    k_p = jnp.pad(k, pad)
    v_p = jnp.pad(v, pad)
    qg = q.reshape(B, nb, BLOCK, N_KV_HEADS, Q_GROUP, HEAD_DIM).transpose(1, 0, 2, 3, 4, 5)
    offs_q = jnp.arange(BLOCK)
    offs_k = jnp.arange(3 * BLOCK) - BLOCK
    rel = offs_k[None, :] - offs_q[:, None]
    in_window = jnp.abs(rel) <= WINDOW
    sink_g = sink.reshape(N_KV_HEADS, Q_GROUP)
    scale = 1.0 / math.sqrt(HEAD_DIM)

    def one_block(args):
        qb, b = args
        start = b * BLOCK
        kb = lax.dynamic_slice_in_dim(k_p, start, 3 * BLOCK, axis=1)
        vb = lax.dynamic_slice_in_dim(v_p, start, 3 * BLOCK, axis=1)
        j = start - BLOCK + jnp.arange(3 * BLOCK)
        valid = in_window & ((j >= 0) & (j < S))[None, :]
        l_loc = jnp.einsum('bqkgd,bskd->bkgqs', qb, kb).astype(jnp.float32) * scale
        l_loc = jnp.where(valid, l_loc, NEG_INF)
        l_ctx = jnp.einsum('bqkgd,bckd->bkgqc', qb, k_c).astype(jnp.float32) * scale
        probs = _softmax_with_sink(jnp.concatenate([l_loc, l_ctx], axis=-1), sink_g)
        p_loc = probs[..., :3 * BLOCK].astype(v.dtype)
        p_ctx = probs[..., 3 * BLOCK:].astype(v.dtype)
        out = (jnp.einsum('bkgqs,bskd->bqkgd', p_loc, vb)
               + jnp.einsum('bkgqc,bckd->bqkgd', p_ctx, v_c))
        return out.reshape(B, BLOCK, Q_WIDTH)

    outs = lax.map(one_block, (qg, jnp.arange(nb)))
    return outs.transpose(1, 0, 2, 3).reshape(B, S, Q_WIDTH)


def _multiscale_pool(p):
    B, L, _ = p.shape
    pg = p.reshape(B, L, N_POOL_GROUPS, POOL_GROUP_DIM).astype(jnp.float32)
    cs = jnp.concatenate([jnp.zeros((B, 1, N_POOL_GROUPS, POOL_GROUP_DIM), jnp.float32),
                          jnp.cumsum(pg, axis=1)], axis=1)
    t = jnp.arange(L)
    outs = []
    for gi, w in enumerate(POOL_WINDOWS):
        lo = jnp.clip(t - w // 2, 0, L)
        hi = jnp.clip(t + w // 2, 0, L)
        cs_g = cs[:, :, gi]
        window_sum = jnp.take(cs_g, hi, axis=1) - jnp.take(cs_g, lo, axis=1)
        cnt = (hi - lo).astype(jnp.float32)
        outs.append(window_sum / cnt[None, :, None] - pg[:, :, gi])
    return jnp.stack(outs, axis=2).astype(p.dtype)


def _merge_branches(attn_o, p, g, w_ab, w_pool, pool_scale, w_out):
    B, L, _ = p.shape
    attn_d = attn_o @ w_ab
    pooled = _multiscale_pool(p)
    pool_d = jnp.einsum('blgc,gcd->blgd', pooled, w_pool).reshape(B, L, D_MODEL) * pool_scale
    g_attn, g_pool = g[..., :D_MODEL], g[..., D_MODEL:]
    merged = jax.nn.sigmoid(g_attn) * attn_d + jax.nn.sigmoid(g_pool) * pool_d
    return merged @ w_out


def _mlp_sublayer(z, shift, scale, gate, w1, w2, g, b):
    u = _modulate(z, shift, scale)
    h = jnp.square(jax.nn.relu(u @ w1)) @ w2
    return _layernorm(ALPHA * z + gate * h, g, b)


def setup_inputs(seed: int = 0) -> dict:
    key = jax.random.key(seed)
    ks = jax.random.split(key, 18)
    f32 = jnp.float32
    n = lambda k, shape, s: jax.random.normal(k, shape, f32) * s
    return {
        "x": n(ks[0], (BATCH, SEQ, D_MODEL), 1.0),
        "c": n(ks[1], (BATCH, D_MODEL), 1.0),
        "ctx": n(ks[2], (BATCH, CTX_LEN, D_MODEL), 1.0),
        "c_ctx": n(ks[3], (D_MODEL,), 1.0),
        "w_ada": n(ks[4], (DEPTH, D_MODEL, N_MOD * D_MODEL), D_MODEL ** -0.5),
        "b_ada": n(ks[5], (DEPTH, N_MOD * D_MODEL), 0.01),
        "w_in": n(ks[6], (DEPTH, D_MODEL, IN_WIDTH), D_MODEL ** -0.5),
        "w_attn_branch": n(ks[7], (DEPTH, Q_WIDTH, D_MODEL), Q_WIDTH ** -0.5),
        "w_pool": n(ks[8], (DEPTH, N_POOL_GROUPS, POOL_GROUP_DIM, D_MODEL // N_POOL_GROUPS), POOL_GROUP_DIM ** -0.5),
        "pool_scale": 1.0 + n(ks[9], (DEPTH, D_MODEL), 0.05),
        "attn_sink": n(ks[10], (DEPTH, N_HEADS), 0.5),
        "w_out": n(ks[11], (DEPTH, D_MODEL, D_MODEL), BETA * D_MODEL ** -0.5),
        "ln1_g": 1.0 + n(ks[12], (DEPTH, D_MODEL), 0.02),
        "ln1_b": n(ks[13], (DEPTH, D_MODEL), 0.02),
        "w_mlp_in": n(ks[14], (DEPTH, D_MODEL, D_FF), D_MODEL ** -0.5),
        "w_mlp_out": n(ks[15], (DEPTH, D_FF, D_MODEL), BETA * D_FF ** -0.5),
        "ln2_g": 1.0 + n(ks[16], (DEPTH, D_MODEL), 0.02),
        "ln2_b": n(ks[17], (DEPTH, D_MODEL), 0.02),
    }


def reference(x, c, ctx, c_ctx, w_ada, b_ada, w_in, w_attn_branch, w_pool, pool_scale,
              attn_sink, w_out, ln1_g, ln1_b, w_mlp_in, w_mlp_out, ln2_g, ln2_b):
    S = x.shape[1]
    ROWS = S // GRID_W
    rows = jnp.repeat(jnp.arange(ROWS), GRID_W)
    cols = jnp.tile(jnp.arange(GRID_W), ROWS)

    for l in range(DEPTH):
        mod = jax.nn.silu(c) @ w_ada[l] + b_ada[l]
        mod_c = jax.nn.silu(c_ctx) @ w_ada[l] + b_ada[l]
        sh1, sc1, g1, sh2, sc2, g2 = [m[:, None, :] for m in jnp.split(mod, N_MOD, axis=-1)]
        csh1, csc1, cg1, csh2, csc2, cg2 = jnp.split(mod_c, N_MOD, axis=-1)

        q_c, k_c, v_c, p_c, gt_c = _split_proj(_modulate(ctx, csh1, csc1) @ w_in[l])

        q, k, v, p, gt = _split_proj(_modulate(x, sh1, sc1) @ w_in[l])
        q = _axial_rope(q, rows, cols)
        k = _axial_rope(k, rows, cols)
        attn_o = _banded_attention(q, k, v, k_c, v_c, attn_sink[l])
        y = _merge_branches(attn_o, p, gt, w_attn_branch[l], w_pool[l], pool_scale[l], w_out[l])
        x = _layernorm(ALPHA * x + g1 * y, ln1_g[l], ln1_b[l])
        x = _mlp_sublayer(x, sh2, sc2, g2, w_mlp_in[l], w_mlp_out[l], ln2_g[l], ln2_b[l])

        if l < DEPTH - 1:
            attn_c = _context_attention(q_c, k_c, v_c, attn_sink[l])
            y_c = _merge_branches(attn_c, p_c, gt_c, w_attn_branch[l], w_pool[l], pool_scale[l], w_out[l])
            ctx = _layernorm(ALPHA * ctx + cg1 * y_c, ln1_g[l], ln1_b[l])
            ctx = _mlp_sublayer(ctx, csh2, csc2, cg2, w_mlp_in[l], w_mlp_out[l], ln2_g[l], ln2_b[l])
    return x
```

```python
import functools
import math

import jax
import jax.numpy as jnp
from jax import lax
from jax.experimental import pallas as pl
from jax.experimental.pallas import tpu as pltpu

F32 = jnp.float32
BF16 = jnp.bfloat16

D_MODEL = 1024
GRID_W = 64
N_HEADS = 16
N_KV_HEADS = 4
HEAD_DIM = 64
WINDOW = 128
BLOCK = 128
ROPE_THETA = 10000.0
POOL_WINDOWS = (2, 4, 8, 16)
POOL_GROUP_DIM = 128
Q_WIDTH = 1024
KV_WIDTH = 256
POOL_WIDTH = 512
GATE_WIDTH = 2048
D_FF = 4096
N_MOD = 6
LN_EPS = 1e-6
NEG_INF = -1e30

LANES = 128
MOD_ROWS = 8
VMEM_LIMIT = 56 * 1024 * 1024


def _layernorm_rows(x):
    mu = jnp.mean(x, axis=-1, keepdims=True)
    xc = x - mu
    var = jnp.mean(xc * xc, axis=-1, keepdims=True)
    return xc * lax.rsqrt(var + LN_EPS)


def _mod_kernel(c_ref, w_ref, b_ref, o_ref):
    c = c_ref[...]
    a = (c * jax.nn.sigmoid(c)).astype(BF16)
    o_ref[...] = jnp.dot(a, w_ref[...].astype(BF16), preferred_element_type=F32) + b_ref[...]


def _mod_call(cc, w_ada, b_ada):
    n = w_ada.shape[1]
    tn = 1024
    return pl.pallas_call(
        _mod_kernel,
        grid=(n // tn,),
        in_specs=[
            pl.BlockSpec((MOD_ROWS, D_MODEL), lambda j: (0, 0)),
            pl.BlockSpec((D_MODEL, tn), lambda j: (0, j)),
            pl.BlockSpec((1, tn), lambda j: (0, j)),
        ],
        out_specs=pl.BlockSpec((MOD_ROWS, tn), lambda j: (0, j)),
        out_shape=jax.ShapeDtypeStruct((MOD_ROWS, n), F32),
        compiler_params=pltpu.CompilerParams(
            dimension_semantics=("arbitrary",), vmem_limit_bytes=VMEM_LIMIT),
        name="mod",
    )(cc, w_ada, b_ada)


def _rope(z, cos, sin_dn, sin_up):
    return (z * cos
            + pltpu.roll(z, LANES - 16, 1) * sin_dn
            + pltpu.roll(z, 16, 1) * sin_up)


def _dup_halves(x):
    r = pltpu.roll(x, 64, 1)
    lo = lax.broadcasted_iota(jnp.int32, x.shape, 1) < 64
    return jnp.where(lo, x, r), jnp.where(lo, r, x)


def _inproj_kernel(x_ref, sh_ref, sc_ref, w_ref, cos_ref, sdn_ref, sup_ref,
                   q_ref, kt_ref, vd_ref, p_ref, g_ref, u_ref):
    x = x_ref[0]
    u_ref[...] = (_layernorm_rows(x) * (1.0 + sc_ref[0]) + sh_ref[0]).astype(BF16)
    cos, sdn, sup = cos_ref[...], sdn_ref[...], sup_ref[...]

    def proj(c0, width):
        return jnp.dot(u_ref[...], w_ref[:, c0:c0 + width], preferred_element_type=F32)

    scale = 1.0 / math.sqrt(HEAD_DIM)
    for c in range(0, Q_WIDTH, 512):
        acc = proj(c, 512)
        for j in range(0, 512, LANES):
            z = _rope(acc[:, j:j + LANES], cos, sdn, sup) * scale
            q_ref[0, :, c + j:c + j + LANES] = z.astype(BF16)

    acc = proj(Q_WIDTH, KV_WIDTH)
    k = jnp.concatenate(
        [_rope(acc[:, j:j + LANES], cos, sdn, sup) for j in range(0, KV_WIDTH, LANES)], axis=1)
    kt_ref[0] = k.T.astype(BF16)

    acc = proj(Q_WIDTH + KV_WIDTH, KV_WIDTH)
    for j in range(KV_WIDTH // LANES):
        a, b = _dup_halves(acc[:, j * LANES:(j + 1) * LANES])
        vd_ref[0, :, (2 * j) * LANES:(2 * j + 1) * LANES] = a.astype(BF16)
        vd_ref[0, :, (2 * j + 1) * LANES:(2 * j + 2) * LANES] = b.astype(BF16)

    p_ref[0] = proj(Q_WIDTH + 2 * KV_WIDTH, POOL_WIDTH).astype(BF16)

    g0 = Q_WIDTH + 2 * KV_WIDTH + POOL_WIDTH
    for c in range(0, GATE_WIDTH, 512):
        g_ref[0, :, c:c + 512] = jax.nn.sigmoid(proj(g0 + c, 512)).astype(BF16)


def _inproj_call(x, mod3, w_in, cos_t, sdn_t, sup_t, tm):
    B, S, _ = x.shape
    grid = (S // tm, B)
    return pl.pallas_call(
        _inproj_kernel,
        grid=grid,
        in_specs=[
            pl.BlockSpec((1, tm, D_MODEL), lambda i, b: (b, i, 0)),
            pl.BlockSpec((1, 1, D_MODEL), lambda i, b: (b, 0, 0)),
            pl.BlockSpec((1, 1, D_MODEL), lambda i, b: (b, 0, 1)),
            pl.BlockSpec(w_in.shape, lambda i, b: (0, 0)),
            pl.BlockSpec((tm, LANES), lambda i, b: (i, 0)),
            pl.BlockSpec((tm, LANES), lambda i, b: (i, 0)),
            pl.BlockSpec((tm, LANES), lambda i, b: (i, 0)),
        ],
        out_specs=[
            pl.BlockSpec((1, tm, Q_WIDTH), lambda i, b: (b, i, 0)),
            pl.BlockSpec((1, KV_WIDTH, tm), lambda i, b: (b, 0, i)),
            pl.BlockSpec((1, tm, 2 * KV_WIDTH), lambda i, b: (b, i, 0)),
            pl.BlockSpec((1, tm, POOL_WIDTH), lambda i, b: (b, i, 0)),
            pl.BlockSpec((1, tm, GATE_WIDTH), lambda i, b: (b, i, 0)),
        ],
        out_shape=[
            jax.ShapeDtypeStruct((B, S, Q_WIDTH), BF16),
            jax.ShapeDtypeStruct((B, KV_WIDTH, S), BF16),
            jax.ShapeDtypeStruct((B, S, 2 * KV_WIDTH), BF16),
            jax.ShapeDtypeStruct((B, S, POOL_WIDTH), BF16),
            jax.ShapeDtypeStruct((B, S, GATE_WIDTH), BF16),
        ],
        scratch_shapes=[pltpu.VMEM((tm, D_MODEL), BF16)],
        compiler_params=pltpu.CompilerParams(
            dimension_semantics=("arbitrary", "arbitrary"), vmem_limit_bytes=VMEM_LIMIT),
        name="inproj",
    )(x, mod3, mod3, w_in, cos_t, sdn_t, sup_t)


def _ctxproj_kernel(x_ref, sh_ref, sc_ref, w_ref, kz_ref, vz_ref):
    u = (_layernorm_rows(x_ref[0]) * (1.0 + sc_ref[0]) + sh_ref[0]).astype(BF16)
    acc = jnp.dot(u, w_ref[...], preferred_element_type=F32)
    kt = acc[:, :KV_WIDTH].T.astype(BF16)
    v = acc[:, KV_WIDTH:]
    C = kt.shape[1]
    zk = jnp.zeros((HEAD_DIM, C), BF16)
    lo = lax.broadcasted_iota(jnp.int32, (C, LANES), 1) < 64
    for g in range(N_KV_HEADS):
        ktg = kt[g * HEAD_DIM:(g + 1) * HEAD_DIM, :]
        kz_ref[0, g] = jnp.concatenate(
            [jnp.concatenate([ktg, zk], axis=0), jnp.concatenate([zk, ktg], axis=0)], axis=1)
        a, b = _dup_halves(v[:, (g // 2) * LANES:(g // 2 + 1) * LANES])
        vg = a if g % 2 == 0 else b
        vz_ref[0, g] = jnp.concatenate(
            [jnp.where(lo, vg, 0.0), jnp.where(lo, 0.0, vg)], axis=0).astype(BF16)


def _ctxproj_call(ctx, mod3, w_kv):
    B, C, _ = ctx.shape
    row = MOD_ROWS // 2
    return pl.pallas_call(
        _ctxproj_kernel,
        grid=(B,),
        in_specs=[
            pl.BlockSpec((1, C, D_MODEL), lambda b: (b, 0, 0)),
            pl.BlockSpec((1, 1, D_MODEL), lambda b: (row, 0, 0)),
            pl.BlockSpec((1, 1, D_MODEL), lambda b: (row, 0, 1)),
            pl.BlockSpec(w_kv.shape, lambda b: (0, 0)),
        ],
        out_specs=[
            pl.BlockSpec((1, N_KV_HEADS, LANES, 2 * C), lambda b: (b, 0, 0, 0)),
            pl.BlockSpec((1, N_KV_HEADS, 2 * C, LANES), lambda b: (b, 0, 0, 0)),
        ],
        out_shape=[
            jax.ShapeDtypeStruct((B, N_KV_HEADS, LANES, 2 * C), BF16),
            jax.ShapeDtypeStruct((B, N_KV_HEADS, 2 * C, LANES), BF16),
        ],
        compiler_params=pltpu.CompilerParams(
            dimension_semantics=("arbitrary",), vmem_limit_bytes=VMEM_LIMIT),
        name="ctxproj",
    )(ctx, mod3, mod3, w_kv)


def _mix_kernel(sink_ref, x_ref, q_ref, ktc_ref, ktp_ref, ktn_ref, vdc_ref, vdp_ref, vdn_ref,
                pc_ref, pp_ref, pn_ref, g_ref, kzc_ref, vzc_ref, g1_ref,
                wab_ref, wpool_ref, pscale_ref, wout_ref, lng_ref, lnb_ref,
                o_ref, ao_ref, band_ref, *, seq_len, tq):
    i = pl.program_id(1)
    n_tiles = pl.num_programs(1)
    nq = tq // BLOCK
    C = kzc_ref.shape[3] // 2
    WIN = 3 * BLOCK

    @pl.when((pl.program_id(0) == 0) & (i == 0))
    def _():
        r = lax.broadcasted_iota(jnp.int32, (tq, tq + 2 * BLOCK), 0)
        c = lax.broadcasted_iota(jnp.int32, (tq, tq + 2 * BLOCK), 1)
        d = c - BLOCK - r
        for gi, w in enumerate(POOL_WINDOWS):
            band_ref[gi] = ((d >= -(w // 2)) & (d <= w // 2 - 1)).astype(BF16)

    first = i == 0
    last = i == n_tiles - 1

    kt_all = jnp.concatenate([ktp_ref[0], ktc_ref[0], ktn_ref[0]], axis=1)
    vd_all = jnp.concatenate([vdp_ref[0], vdc_ref[0], vdn_ref[0]], axis=0)

    row = lax.broadcasted_iota(jnp.int32, (BLOCK, WIN), 0)
    col = lax.broadcasted_iota(jnp.int32, (BLOCK, WIN), 1)
    rel = col - BLOCK - row
    band = (rel >= -WINDOW) & (rel <= WINDOW)
    lane_lo = lax.broadcasted_iota(jnp.int32, (WIN, LANES), 1) < 64
    out_lo = lax.broadcasted_iota(jnp.int32, (BLOCK, LANES), 1) < 64
    zk = jnp.zeros((HEAD_DIM, WIN), BF16)

    for r in range(nq):
        key0 = (i * nq + r - 1) * BLOCK + col
        valid = band & (key0 >= 0) & (key0 < seq_len)
        for g in range(N_KV_HEADS):
            ktg = kt_all[g * HEAD_DIM:(g + 1) * HEAD_DIM, r * BLOCK:r * BLOCK + WIN]
            kz = jnp.concatenate(
                [jnp.concatenate([ktg, zk], axis=0), jnp.concatenate([zk, ktg], axis=0)], axis=1)
            vdg = vd_all[r * BLOCK:r * BLOCK + WIN, g * LANES:(g + 1) * LANES]
            vz = jnp.concatenate(
                [jnp.where(lane_lo, vdg, 0), jnp.where(lane_lo, 0, vdg)], axis=0).astype(BF16)
            for jj in range(2):
                qc = (2 * g + jj) * LANES
                q2 = q_ref[0, r * BLOCK:(r + 1) * BLOCK, qc:qc + LANES]
                s_loc = jnp.dot(q2, kz, preferred_element_type=F32)
                s_ctx = jnp.dot(q2, kzc_ref[0, g], preferred_element_type=F32)
                es, ec, inv = [], [], []
                for hh in range(2):
                    sink = sink_ref[2 * (2 * g + jj) + hh]
                    sl = jnp.where(valid, s_loc[:, hh * WIN:(hh + 1) * WIN], NEG_INF)
                    sc = s_ctx[:, hh * C:(hh + 1) * C]
                    m = jnp.maximum(jnp.max(sl, axis=-1, keepdims=True),
                                    jnp.max(sc, axis=-1, keepdims=True))
                    m = jnp.maximum(m, sink)
                    el = jnp.exp(sl - m)
                    ee = jnp.exp(sc - m)
                    den = (jnp.sum(el, axis=-1, keepdims=True) + jnp.sum(ee, axis=-1, keepdims=True)
                           + jnp.exp(sink - m))
                    es.append(el.astype(BF16))
                    ec.append(ee.astype(BF16))
                    inv.append(1.0 / den)
                o2 = (jnp.dot(jnp.concatenate(es, axis=1), vz, preferred_element_type=F32)
                      + jnp.dot(jnp.concatenate(ec, axis=1), vzc_ref[0, g],
                                preferred_element_type=F32))
                o2 = o2 * jnp.where(out_lo, inv[0], inv[1])
                ao_ref[r * BLOCK:(r + 1) * BLOCK, qc:qc + LANES] = o2.astype(BF16)

    attn_d = jnp.dot(ao_ref[...], wab_ref[...], preferred_element_type=F32)

    p_cur = pc_ref[0]
    p_ext = jnp.concatenate(
        [jnp.where(first, jnp.zeros_like(pp_ref[0]), pp_ref[0]), p_cur,
         jnp.where(last, jnp.zeros_like(pn_ref[0]), pn_ref[0])], axis=0)
    t = i * tq + lax.broadcasted_iota(jnp.int32, (tq, POOL_GROUP_DIM), 0)
    pool_parts = []
    for gi, w in enumerate(POOL_WINDOWS):
        sl = slice(gi * POOL_GROUP_DIM, (gi + 1) * POOL_GROUP_DIM)
        wsum = jnp.dot(band_ref[gi], p_ext[:, sl], preferred_element_type=F32)
        cnt = (jnp.minimum(t + w // 2, seq_len) - jnp.maximum(t - w // 2, 0)).astype(F32)
        pooled = wsum / cnt - p_cur[:, sl].astype(F32)
        pool_parts.append(jnp.dot(pooled.astype(BF16), wpool_ref[gi], preferred_element_type=F32))
    pool_d = jnp.concatenate(pool_parts, axis=1) * pscale_ref[...]

    merged = (g_ref[0, :, :D_MODEL].astype(F32) * attn_d
              + g_ref[0, :, D_MODEL:].astype(F32) * pool_d)
    y = jnp.dot(merged.astype(BF16), wout_ref[...], preferred_element_type=F32)
    alpha = 2.0 ** 0.25
    z = alpha * x_ref[0] + g1_ref[0] * y
    o_ref[0] = _layernorm_rows(z) * lng_ref[...] + lnb_ref[...]


def _mix_call(sink, x, q, kt, vd, p, g, kzc, vzc, mod3, w_ab, w_pool, pool_scale, w_out,
              ln_g, ln_b, tq):
    B, S, _ = x.shape
    nb = S // BLOCK
    nq = tq // BLOCK
    C = kzc.shape[3] // 2
    const2 = lambda b, i: (0, 0)
    kernel = functools.partial(_mix_kernel, seq_len=S, tq=tq)
    return pl.pallas_call(
        kernel,
        grid=(B, S // tq),
        in_specs=[
            pl.BlockSpec(memory_space=pltpu.SMEM),
            pl.BlockSpec((1, tq, D_MODEL), lambda b, i: (b, i, 0)),
            pl.BlockSpec((1, tq, Q_WIDTH), lambda b, i: (b, i, 0)),
            pl.BlockSpec((1, KV_WIDTH, tq), lambda b, i: (b, 0, i)),
            pl.BlockSpec((1, KV_WIDTH, BLOCK), lambda b, i: (b, 0, jnp.maximum(i * nq - 1, 0))),
            pl.BlockSpec((1, KV_WIDTH, BLOCK), lambda b, i: (b, 0, jnp.minimum((i + 1) * nq, nb - 1))),
            pl.BlockSpec((1, tq, 2 * KV_WIDTH), lambda b, i: (b, i, 0)),
            pl.BlockSpec((1, BLOCK, 2 * KV_WIDTH), lambda b, i: (b, jnp.maximum(i * nq - 1, 0), 0)),
            pl.BlockSpec((1, BLOCK, 2 * KV_WIDTH), lambda b, i: (b, jnp.minimum((i + 1) * nq, nb - 1), 0)),
            pl.BlockSpec((1, tq, POOL_WIDTH), lambda b, i: (b, i, 0)),
            pl.BlockSpec((1, BLOCK, POOL_WIDTH), lambda b, i: (b, jnp.maximum(i * nq - 1, 0), 0)),
            pl.BlockSpec((1, BLOCK, POOL_WIDTH), lambda b, i: (b, jnp.minimum((i + 1) * nq, nb - 1), 0)),
            pl.BlockSpec((1, tq, GATE_WIDTH), lambda b, i: (b, i, 0)),
            pl.BlockSpec((1, N_KV_HEADS, LANES, 2 * C), lambda b, i: (b, 0, 0, 0)),
            pl.BlockSpec((1, N_KV_HEADS, 2 * C, LANES), lambda b, i: (b, 0, 0, 0)),
            pl.BlockSpec((1, 1, D_MODEL), lambda b, i: (b, 0, 2)),
            pl.BlockSpec(w_ab.shape, const2),
            pl.BlockSpec(w_pool.shape, lambda b, i: (0, 0, 0)),
            pl.BlockSpec((1, D_MODEL), const2),
            pl.BlockSpec(w_out.shape, const2),
            pl.BlockSpec((1, D_MODEL), const2),
            pl.BlockSpec((1, D_MODEL), const2),
        ],
        out_specs=pl.BlockSpec((1, tq, D_MODEL), lambda b, i: (b, i, 0)),
        out_shape=jax.ShapeDtypeStruct((B, S, D_MODEL), F32),
        scratch_shapes=[
            pltpu.VMEM((tq, Q_WIDTH), BF16),
            pltpu.VMEM((len(POOL_WINDOWS), tq, tq + 2 * BLOCK), BF16),
        ],
        compiler_params=pltpu.CompilerParams(
            dimension_semantics=("arbitrary", "arbitrary"), vmem_limit_bytes=VMEM_LIMIT),
        name="mix",
    )(sink, x, q, kt, kt, kt, vd, vd, vd, p, p, p, g, kzc, vzc, mod3,
      w_ab, w_pool, pool_scale, w_out, ln_g, ln_b)


def _mlp_kernel(x_ref, sh_ref, sc_ref, gt_ref, w1_ref, w2_ref, lng_ref, lnb_ref,
                o_ref, u_ref, acc_ref, *, ck):
    x = x_ref[0]
    u_ref[...] = (_layernorm_rows(x) * (1.0 + sc_ref[0]) + sh_ref[0]).astype(BF16)
    for n, c in enumerate(range(0, D_FF, ck)):
        h = jnp.dot(u_ref[...], w1_ref[:, c:c + ck], preferred_element_type=F32)
        h = jnp.square(jnp.maximum(h, 0.0)).astype(BF16)
        part = jnp.dot(h, w2_ref[c:c + ck, :], preferred_element_type=F32)
        if n == 0:
            acc_ref[...] = part
        else:
            acc_ref[...] += part
    alpha = 2.0 ** 0.25
    z = alpha * x + gt_ref[0] * acc_ref[...]
    o_ref[0] = _layernorm_rows(z) * lng_ref[...] + lnb_ref[...]


def _mlp_call(x, mod3, w1, w2, ln_g, ln_b, tm, ck):
    B, S, _ = x.shape
    const2 = lambda b, i: (0, 0)
    return pl.pallas_call(
        functools.partial(_mlp_kernel, ck=ck),
        grid=(B, S // tm),
        in_specs=[
            pl.BlockSpec((1, tm, D_MODEL), lambda b, i: (b, i, 0)),
            pl.BlockSpec((1, 1, D_MODEL), lambda b, i: (b, 0, 3)),
            pl.BlockSpec((1, 1, D_MODEL), lambda b, i: (b, 0, 4)),
            pl.BlockSpec((1, 1, D_MODEL), lambda b, i: (b, 0, 5)),
            pl.BlockSpec(w1.shape, const2, pipeline_mode=pl.Buffered(1)),
            pl.BlockSpec(w2.shape, const2, pipeline_mode=pl.Buffered(1)),
            pl.BlockSpec((1, D_MODEL), const2),
            pl.BlockSpec((1, D_MODEL), const2),
        ],
        out_specs=pl.BlockSpec((1, tm, D_MODEL), lambda b, i: (b, i, 0)),
        out_shape=jax.ShapeDtypeStruct((B, S, D_MODEL), F32),
        scratch_shapes=[pltpu.VMEM((tm, D_MODEL), BF16), pltpu.VMEM((tm, D_MODEL), F32)],
        compiler_params=pltpu.CompilerParams(
            dimension_semantics=("arbitrary", "arbitrary"), vmem_limit_bytes=VMEM_LIMIT),
        name="mlp",
    )(x, mod3, mod3, mod3, w1, w2, ln_g, ln_b)


def _rope_tables(seq_len):
    quarter = HEAD_DIM // 4
    inv = 1.0 / (ROPE_THETA ** (jnp.arange(quarter, dtype=F32) / quarter))
    t = jnp.arange(seq_len)
    ang_r = (t // GRID_W).astype(F32)[:, None] * inv[None, :]
    ang_c = (t % GRID_W).astype(F32)[:, None] * inv[None, :]
    zero = jnp.zeros_like(ang_r)
    cos = jnp.concatenate([jnp.cos(ang_r)] * 2 + [jnp.cos(ang_c)] * 2, axis=1)
    sdn = jnp.concatenate([-jnp.sin(ang_r), zero, -jnp.sin(ang_c), zero], axis=1)
    sup = jnp.concatenate([zero, jnp.sin(ang_r), zero, jnp.sin(ang_c)], axis=1)
    return tuple(jnp.tile(a, (1, LANES // HEAD_DIM)) for a in (cos, sdn, sup))


def kernel(x, c, ctx, c_ctx, w_ada, b_ada, w_in, w_attn_branch, w_pool, pool_scale, attn_sink,
           w_out, ln1_g, ln1_b, w_mlp_in, w_mlp_out, ln2_g, ln2_b):
    assert w_ada.shape[0] == 1, "single-layer block"
    B, S, _ = x.shape
    assert B + 1 <= MOD_ROWS
    cc = jnp.zeros((MOD_ROWS, D_MODEL), F32).at[:B].set(c).at[MOD_ROWS // 2].set(c_ctx)
    mod = _mod_call(cc, w_ada[0], b_ada[0][None, :])
    mod3 = mod.reshape(MOD_ROWS, 1, N_MOD * D_MODEL)

    w_in_b = w_in[0].astype(BF16)
    cos_t, sdn_t, sup_t = _rope_tables(S)
    q, kt, vd, p, g = _inproj_call(x, mod3, w_in_b, cos_t, sdn_t, sup_t, tm=512)
    kzc, vzc = _ctxproj_call(ctx, mod3, w_in_b[:, Q_WIDTH:Q_WIDTH + 2 * KV_WIDTH])

    x1 = _mix_call(attn_sink[0], x, q, kt, vd, p, g, kzc, vzc, mod3,
                   w_attn_branch[0].astype(BF16), w_pool[0].astype(BF16), pool_scale[0][None, :],
                   w_out[0].astype(BF16), ln1_g[0][None, :], ln1_b[0][None, :], tq=256)
    return _mlp_call(x1, mod3, w_mlp_in[0].astype(BF16), w_mlp_out[0].astype(BF16),
                     ln2_g[0][None, :], ln2_b[0][None, :], tm=512, ck=512)
```

```python
import functools
import math

import jax
import jax.numpy as jnp
from jax import lax
from jax.experimental import pallas as pl
from jax.experimental.pallas import tpu as pltpu

F32 = jnp.float32
BF16 = jnp.bfloat16

D_MODEL = 1024
GRID_W = 64
N_HEADS = 16
N_KV_HEADS = 4
HEAD_DIM = 64
WINDOW = 128
BLOCK = 128
ROPE_THETA = 10000.0
POOL_WINDOWS = (2, 4, 8, 16)
POOL_GROUP_DIM = 128
Q_WIDTH = 1024
KV_WIDTH = 256
POOL_WIDTH = 512
GATE_WIDTH = 2048
D_FF = 4096
N_MOD = 6
LN_EPS = 1e-6
NEG_INF = -1e30
LOG2E = math.log2(math.e)

LANES = 128
MOD_ROWS = 8
VMEM_LIMIT = 56 * 1024 * 1024


def _layernorm_rows(x):
    mu = jnp.mean(x, axis=-1, keepdims=True)
    xc = x - mu
    var = jnp.mean(xc * xc, axis=-1, keepdims=True)
    return xc * lax.rsqrt(var + LN_EPS)


def _mod_kernel(c_ref, w_ref, b_ref, o_ref):
    c = c_ref[...]
    a = (c * jax.nn.sigmoid(c)).astype(BF16)
    o_ref[...] = jnp.dot(a, w_ref[...].astype(BF16), preferred_element_type=F32) + b_ref[...]


def _mod_call(cc, w_ada, b_ada):
    n = w_ada.shape[1]
    tn = 1024
    return pl.pallas_call(
        _mod_kernel,
        grid=(n // tn,),
        in_specs=[
            pl.BlockSpec((MOD_ROWS, D_MODEL), lambda j: (0, 0)),
            pl.BlockSpec((D_MODEL, tn), lambda j: (0, j)),
            pl.BlockSpec((1, tn), lambda j: (0, j)),
        ],
        out_specs=pl.BlockSpec((MOD_ROWS, tn), lambda j: (0, j)),
        out_shape=jax.ShapeDtypeStruct((MOD_ROWS, n), F32),
        compiler_params=pltpu.CompilerParams(
            dimension_semantics=("arbitrary",), vmem_limit_bytes=VMEM_LIMIT),
        name="mod",
    )(cc, w_ada, b_ada)


def _rope(z, cos, sin_dn, sin_up):
    return (z * cos
            + pltpu.roll(z, LANES - 16, 1) * sin_dn
            + pltpu.roll(z, 16, 1) * sin_up)


def _dup_halves(x):
    r = pltpu.roll(x, 64, 1)
    lo = lax.broadcasted_iota(jnp.int32, x.shape, 1) < 64
    return jnp.where(lo, x, r), jnp.where(lo, r, x)


def _inproj_kernel(x_ref, sh_ref, sc_ref, w_ref, cos_ref, sdn_ref, sup_ref,
                   q_ref, kt_ref, vd_ref, p_ref, g_ref, u_ref):
    x = x_ref[0]
    u_ref[...] = (_layernorm_rows(x) * (1.0 + sc_ref[0]) + sh_ref[0]).astype(BF16)
    cos, sdn, sup = cos_ref[...], sdn_ref[...], sup_ref[...]

    def proj(c0, width):
        return jnp.dot(u_ref[...], w_ref[:, c0:c0 + width], preferred_element_type=F32)

    scale = LOG2E / math.sqrt(HEAD_DIM)
    for c in range(0, Q_WIDTH, 512):
        acc = proj(c, 512)
        for j in range(0, 512, LANES):
            z = _rope(acc[:, j:j + LANES], cos, sdn, sup) * scale
            q_ref[0, :, c + j:c + j + LANES] = z.astype(BF16)

    acc = proj(Q_WIDTH, KV_WIDTH)
    k = jnp.concatenate(
        [_rope(acc[:, j:j + LANES], cos, sdn, sup) for j in range(0, KV_WIDTH, LANES)], axis=1)
    kt_ref[0] = k.T.astype(BF16)

    acc = proj(Q_WIDTH + KV_WIDTH, KV_WIDTH)
    for j in range(KV_WIDTH // LANES):
        a, b = _dup_halves(acc[:, j * LANES:(j + 1) * LANES])
        vd_ref[0, :, (2 * j) * LANES:(2 * j + 1) * LANES] = a.astype(BF16)
        vd_ref[0, :, (2 * j + 1) * LANES:(2 * j + 2) * LANES] = b.astype(BF16)

    p_ref[0] = proj(Q_WIDTH + 2 * KV_WIDTH, POOL_WIDTH).astype(BF16)

    g0 = Q_WIDTH + 2 * KV_WIDTH + POOL_WIDTH
    for c in range(0, GATE_WIDTH, 512):
        g_ref[0, :, c:c + 512] = jax.nn.sigmoid(proj(g0 + c, 512)).astype(BF16)


def _inproj_call(x, mod3, w_in, cos_t, sdn_t, sup_t, tm):
    B, S, _ = x.shape
    grid = (S // tm, B)
    return pl.pallas_call(
        _inproj_kernel,
        grid=grid,
        in_specs=[
            pl.BlockSpec((1, tm, D_MODEL), lambda i, b: (b, i, 0)),
            pl.BlockSpec((1, 1, D_MODEL), lambda i, b: (b, 0, 0)),
            pl.BlockSpec((1, 1, D_MODEL), lambda i, b: (b, 0, 1)),
            pl.BlockSpec(w_in.shape, lambda i, b: (0, 0)),
            pl.BlockSpec((tm, LANES), lambda i, b: (i, 0)),
            pl.BlockSpec((tm, LANES), lambda i, b: (i, 0)),
            pl.BlockSpec((tm, LANES), lambda i, b: (i, 0)),
        ],
        out_specs=[
            pl.BlockSpec((1, tm, Q_WIDTH), lambda i, b: (b, i, 0)),
            pl.BlockSpec((1, KV_WIDTH, tm), lambda i, b: (b, 0, i)),
            pl.BlockSpec((1, tm, 2 * KV_WIDTH), lambda i, b: (b, i, 0)),
            pl.BlockSpec((1, tm, POOL_WIDTH), lambda i, b: (b, i, 0)),
            pl.BlockSpec((1, tm, GATE_WIDTH), lambda i, b: (b, i, 0)),
        ],
        out_shape=[
            jax.ShapeDtypeStruct((B, S, Q_WIDTH), BF16),
            jax.ShapeDtypeStruct((B, KV_WIDTH, S), BF16),
            jax.ShapeDtypeStruct((B, S, 2 * KV_WIDTH), BF16),
            jax.ShapeDtypeStruct((B, S, POOL_WIDTH), BF16),
            jax.ShapeDtypeStruct((B, S, GATE_WIDTH), BF16),
        ],
        scratch_shapes=[pltpu.VMEM((tm, D_MODEL), BF16)],
        compiler_params=pltpu.CompilerParams(
            dimension_semantics=("arbitrary", "arbitrary"), vmem_limit_bytes=VMEM_LIMIT),
        name="inproj",
    )(x, mod3, mod3, w_in, cos_t, sdn_t, sup_t)


def _ctxproj_kernel(x_ref, sh_ref, sc_ref, w_ref, kz_ref, vz_ref):
    u = (_layernorm_rows(x_ref[0]) * (1.0 + sc_ref[0]) + sh_ref[0]).astype(BF16)
    acc = jnp.dot(u, w_ref[...], preferred_element_type=F32)
    kt = acc[:, :KV_WIDTH].T.astype(BF16)
    v = acc[:, KV_WIDTH:]
    C = kt.shape[1]
    zk = jnp.zeros((HEAD_DIM, C), BF16)
    lo = lax.broadcasted_iota(jnp.int32, (C, LANES), 1) < 64
    ones_lo = lo.astype(F32)
    for g in range(N_KV_HEADS):
        ktg = kt[g * HEAD_DIM:(g + 1) * HEAD_DIM, :]
        kz_ref[0, g] = jnp.concatenate(
            [jnp.concatenate([ktg, zk], axis=0), jnp.concatenate([zk, ktg], axis=0)], axis=1)
        a, b = _dup_halves(v[:, (g // 2) * LANES:(g // 2 + 1) * LANES])
        vg = a if g % 2 == 0 else b
        vz_ref[0, g] = jnp.concatenate(
            [jnp.concatenate([jnp.where(lo, vg, 0.0), ones_lo], axis=1),
             jnp.concatenate([jnp.where(lo, 0.0, vg), 1.0 - ones_lo], axis=1)],
            axis=0).astype(BF16)


def _ctxproj_call(ctx, mod3, w_kv):
    B, C, _ = ctx.shape
    row = MOD_ROWS // 2
    return pl.pallas_call(
        _ctxproj_kernel,
        grid=(B,),
        in_specs=[
            pl.BlockSpec((1, C, D_MODEL), lambda b: (b, 0, 0)),
            pl.BlockSpec((1, 1, D_MODEL), lambda b: (row, 0, 0)),
            pl.BlockSpec((1, 1, D_MODEL), lambda b: (row, 0, 1)),
            pl.BlockSpec(w_kv.shape, lambda b: (0, 0)),
        ],
        out_specs=[
            pl.BlockSpec((1, N_KV_HEADS, LANES, 2 * C), lambda b: (b, 0, 0, 0)),
            pl.BlockSpec((1, N_KV_HEADS, 2 * C, 2 * LANES), lambda b: (b, 0, 0, 0)),
        ],
        out_shape=[
            jax.ShapeDtypeStruct((B, N_KV_HEADS, LANES, 2 * C), BF16),
            jax.ShapeDtypeStruct((B, N_KV_HEADS, 2 * C, 2 * LANES), BF16),
        ],
        compiler_params=pltpu.CompilerParams(
            dimension_semantics=("arbitrary",), vmem_limit_bytes=VMEM_LIMIT),
        name="ctxproj",
    )(ctx, mod3, mod3, w_kv)


def _mix_kernel(sink_ref, x_ref, q_ref, ktc_ref, ktp_ref, ktn_ref, vdc_ref, vdp_ref, vdn_ref,
                pc_ref, pp_ref, pn_ref, g_ref, kzc_ref, vzc_ref, g1_ref,
                wab_ref, wpool_ref, pscale_ref, wout_ref, lng_ref, lnb_ref,
                o_ref, ao_ref, band_ref, *, seq_len, tq):
    i = pl.program_id(1)
    n_tiles = pl.num_programs(1)
    nq = tq // BLOCK
    C = kzc_ref.shape[3] // 2
    WIN = 3 * BLOCK

    @pl.when((pl.program_id(0) == 0) & (i == 0))
    def _():
        r = lax.broadcasted_iota(jnp.int32, (tq, tq + 2 * BLOCK), 0)
        c = lax.broadcasted_iota(jnp.int32, (tq, tq + 2 * BLOCK), 1)
        d = c - BLOCK - r
        for gi, w in enumerate(POOL_WINDOWS):
            band_ref[gi] = ((d >= -(w // 2)) & (d <= w // 2 - 1)).astype(BF16)

    first = i == 0
    last = i == n_tiles - 1

    kt_all = jnp.concatenate([ktp_ref[0], ktc_ref[0], ktn_ref[0]], axis=1)
    vd_all = jnp.concatenate([vdp_ref[0], vdc_ref[0], vdn_ref[0]], axis=0)

    row = lax.broadcasted_iota(jnp.int32, (2 * BLOCK, BLOCK), 0) % BLOCK
    col = lax.broadcasted_iota(jnp.int32, (2 * BLOCK, BLOCK), 1)
    upper = col >= row
    lower = col <= row
    top_rows = lax.broadcasted_iota(jnp.int32, (2 * BLOCK, 1), 0) < BLOCK
    lane_lo = lax.broadcasted_iota(jnp.int32, (WIN, LANES), 1) < 64
    out_lo = lax.broadcasted_iota(jnp.int32, (2 * BLOCK, LANES), 1) < 64
    ones_lo = lane_lo.astype(BF16)
    ones_hi = 1 - ones_lo
    zk = jnp.zeros((HEAD_DIM, WIN), BF16)

    for r in range(nq):
        blk = i * nq + r
        valid_a = upper & (blk > 0)
        valid_c = lower & (blk < seq_len // BLOCK - 1)
        rows = slice(r * BLOCK, (r + 1) * BLOCK)
        for g in range(N_KV_HEADS):
            ktg = kt_all[g * HEAD_DIM:(g + 1) * HEAD_DIM, r * BLOCK:r * BLOCK + WIN]
            kz = jnp.concatenate(
                [jnp.concatenate([ktg, zk], axis=0), jnp.concatenate([zk, ktg], axis=0)], axis=1)
            vdg = vd_all[r * BLOCK:r * BLOCK + WIN, g * LANES:(g + 1) * LANES]
            vz = jnp.concatenate(
                [jnp.concatenate([jnp.where(lane_lo, vdg, 0).astype(BF16), ones_lo], axis=1),
                 jnp.concatenate([jnp.where(lane_lo, 0, vdg).astype(BF16), ones_hi], axis=1)],
                axis=0)
            qc = 2 * g * LANES
            q4 = jnp.concatenate([q_ref[0, rows, qc:qc + LANES],
                                  q_ref[0, rows, qc + LANES:qc + 2 * LANES]], axis=0)
            s_loc = jnp.dot(q4, kz, preferred_element_type=F32)
            s_ctx = jnp.dot(q4, kzc_ref[0, g], preferred_element_type=F32)
            e_loc, e_ctx, e_sink = [], [], []
            for hh in range(2):
                sink = jnp.where(top_rows, sink_ref[4 * g + hh], sink_ref[4 * g + 2 + hh]) * LOG2E
                a = jnp.where(valid_a, s_loc[:, hh * WIN:hh * WIN + BLOCK], NEG_INF)
                b = s_loc[:, hh * WIN + BLOCK:hh * WIN + 2 * BLOCK]
                c = jnp.where(valid_c, s_loc[:, hh * WIN + 2 * BLOCK:(hh + 1) * WIN], NEG_INF)
                parts = [a, b, c] + [s_ctx[:, hh * C + j:hh * C + j + LANES]
                                     for j in range(0, C, LANES)]
                mx = functools.reduce(jnp.maximum, parts)
                m = jnp.maximum(jnp.max(mx, axis=-1, keepdims=True), sink)
                e_loc += [jnp.exp2(t - m).astype(BF16) for t in parts[:3]]
                e_ctx += [jnp.exp2(t - m).astype(BF16) for t in parts[3:]]
                e_sink.append(jnp.exp2(sink - m))
            o = (jnp.dot(jnp.concatenate(e_loc, axis=1), vz, preferred_element_type=F32)
                 + jnp.dot(jnp.concatenate(e_ctx, axis=1), vzc_ref[0, g],
                           preferred_element_type=F32))
            den = o[:, LANES:] + jnp.where(out_lo, e_sink[0], e_sink[1])
            o2 = (o[:, :LANES] / den).astype(BF16)
            ao_ref[rows, qc:qc + LANES] = o2[:BLOCK]
            ao_ref[rows, qc + LANES:qc + 2 * LANES] = o2[BLOCK:]

    attn_d = jnp.dot(ao_ref[...], wab_ref[...], preferred_element_type=F32)

    p_cur = pc_ref[0]
    p_ext = jnp.concatenate(
        [jnp.where(first, jnp.zeros_like(pp_ref[0]), pp_ref[0]), p_cur,
         jnp.where(last, jnp.zeros_like(pn_ref[0]), pn_ref[0])], axis=0)
    t = i * tq + lax.broadcasted_iota(jnp.int32, (tq, POOL_GROUP_DIM), 0)
    pool_parts = []
    for gi, w in enumerate(POOL_WINDOWS):
        sl = slice(gi * POOL_GROUP_DIM, (gi + 1) * POOL_GROUP_DIM)
        wsum = jnp.dot(band_ref[gi], p_ext[:, sl], preferred_element_type=F32)
        cnt = (jnp.minimum(t + w // 2, seq_len) - jnp.maximum(t - w // 2, 0)).astype(F32)
        pooled = wsum / cnt - p_cur[:, sl].astype(F32)
        pool_parts.append(jnp.dot(pooled.astype(BF16), wpool_ref[gi], preferred_element_type=F32))
    pool_d = jnp.concatenate(pool_parts, axis=1) * pscale_ref[...]

    merged = (g_ref[0, :, :D_MODEL].astype(F32) * attn_d
              + g_ref[0, :, D_MODEL:].astype(F32) * pool_d)
    y = jnp.dot(merged.astype(BF16), wout_ref[...], preferred_element_type=F32)
    alpha = 2.0 ** 0.25
    z = alpha * x_ref[0] + g1_ref[0] * y
    o_ref[0] = _layernorm_rows(z) * lng_ref[...] + lnb_ref[...]


def _mix_call(sink, x, q, kt, vd, p, g, kzc, vzc, mod3, w_ab, w_pool, pool_scale, w_out,
              ln_g, ln_b, tq):
    B, S, _ = x.shape
    nb = S // BLOCK
    nq = tq // BLOCK
    C = kzc.shape[3] // 2
    const2 = lambda b, i: (0, 0)
    kernel = functools.partial(_mix_kernel, seq_len=S, tq=tq)
    return pl.pallas_call(
        kernel,
        grid=(B, S // tq),
        in_specs=[
            pl.BlockSpec(memory_space=pltpu.SMEM),
            pl.BlockSpec((1, tq, D_MODEL), lambda b, i: (b, i, 0)),
            pl.BlockSpec((1, tq, Q_WIDTH), lambda b, i: (b, i, 0)),
            pl.BlockSpec((1, KV_WIDTH, tq), lambda b, i: (b, 0, i)),
            pl.BlockSpec((1, KV_WIDTH, BLOCK), lambda b, i: (b, 0, jnp.maximum(i * nq - 1, 0))),
            pl.BlockSpec((1, KV_WIDTH, BLOCK), lambda b, i: (b, 0, jnp.minimum((i + 1) * nq, nb - 1))),
            pl.BlockSpec((1, tq, 2 * KV_WIDTH), lambda b, i: (b, i, 0)),
            pl.BlockSpec((1, BLOCK, 2 * KV_WIDTH), lambda b, i: (b, jnp.maximum(i * nq - 1, 0), 0)),
            pl.BlockSpec((1, BLOCK, 2 * KV_WIDTH), lambda b, i: (b, jnp.minimum((i + 1) * nq, nb - 1), 0)),
            pl.BlockSpec((1, tq, POOL_WIDTH), lambda b, i: (b, i, 0)),
            pl.BlockSpec((1, BLOCK, POOL_WIDTH), lambda b, i: (b, jnp.maximum(i * nq - 1, 0), 0)),
            pl.BlockSpec((1, BLOCK, POOL_WIDTH), lambda b, i: (b, jnp.minimum((i + 1) * nq, nb - 1), 0)),
            pl.BlockSpec((1, tq, GATE_WIDTH), lambda b, i: (b, i, 0)),
            pl.BlockSpec((1, N_KV_HEADS, LANES, 2 * C), lambda b, i: (b, 0, 0, 0)),
            pl.BlockSpec((1, N_KV_HEADS, 2 * C, 2 * LANES), lambda b, i: (b, 0, 0, 0)),
            pl.BlockSpec((1, 1, D_MODEL), lambda b, i: (b, 0, 2)),
            pl.BlockSpec(w_ab.shape, const2),
            pl.BlockSpec(w_pool.shape, lambda b, i: (0, 0, 0)),
            pl.BlockSpec((1, D_MODEL), const2),
            pl.BlockSpec(w_out.shape, const2),
            pl.BlockSpec((1, D_MODEL), const2),
            pl.BlockSpec((1, D_MODEL), const2),
        ],
        out_specs=pl.BlockSpec((1, tq, D_MODEL), lambda b, i: (b, i, 0)),
        out_shape=jax.ShapeDtypeStruct((B, S, D_MODEL), F32),
        scratch_shapes=[
            pltpu.VMEM((tq, Q_WIDTH), BF16),
            pltpu.VMEM((len(POOL_WINDOWS), tq, tq + 2 * BLOCK), BF16),
        ],
        compiler_params=pltpu.CompilerParams(
            dimension_semantics=("arbitrary", "arbitrary"), vmem_limit_bytes=VMEM_LIMIT),
        name="mix",
    )(sink, x, q, kt, kt, kt, vd, vd, vd, p, p, p, g, kzc, vzc, mod3,
      w_ab, w_pool, pool_scale, w_out, ln_g, ln_b)


def _mlp_kernel(x_ref, sh_ref, sc_ref, gt_ref, w1_ref, w2_ref, lng_ref, lnb_ref,
                o_ref, u_ref, acc_ref, *, ck):
    x = x_ref[0]
    u_ref[...] = (_layernorm_rows(x) * (1.0 + sc_ref[0]) + sh_ref[0]).astype(BF16)
    for n, c in enumerate(range(0, D_FF, ck)):
        h = jnp.dot(u_ref[...], w1_ref[:, c:c + ck], preferred_element_type=F32)
        h = jnp.square(jnp.maximum(h, 0.0)).astype(BF16)
        part = jnp.dot(h, w2_ref[c:c + ck, :], preferred_element_type=F32)
        if n == 0:
            acc_ref[...] = part
        else:
            acc_ref[...] += part
    alpha = 2.0 ** 0.25
    z = alpha * x + gt_ref[0] * acc_ref[...]
    o_ref[0] = _layernorm_rows(z) * lng_ref[...] + lnb_ref[...]


def _mlp_call(x, mod3, w1, w2, ln_g, ln_b, tm, ck):
    B, S, _ = x.shape
    const2 = lambda b, i: (0, 0)
    return pl.pallas_call(
        functools.partial(_mlp_kernel, ck=ck),
        grid=(B, S // tm),
        in_specs=[
            pl.BlockSpec((1, tm, D_MODEL), lambda b, i: (b, i, 0)),
            pl.BlockSpec((1, 1, D_MODEL), lambda b, i: (b, 0, 3)),
            pl.BlockSpec((1, 1, D_MODEL), lambda b, i: (b, 0, 4)),
            pl.BlockSpec((1, 1, D_MODEL), lambda b, i: (b, 0, 5)),
            pl.BlockSpec(w1.shape, const2, pipeline_mode=pl.Buffered(1)),
            pl.BlockSpec(w2.shape, const2, pipeline_mode=pl.Buffered(1)),
            pl.BlockSpec((1, D_MODEL), const2),
            pl.BlockSpec((1, D_MODEL), const2),
        ],
        out_specs=pl.BlockSpec((1, tm, D_MODEL), lambda b, i: (b, i, 0)),
        out_shape=jax.ShapeDtypeStruct((B, S, D_MODEL), F32),
        scratch_shapes=[pltpu.VMEM((tm, D_MODEL), BF16), pltpu.VMEM((tm, D_MODEL), F32)],
        compiler_params=pltpu.CompilerParams(
            dimension_semantics=("arbitrary", "arbitrary"), vmem_limit_bytes=VMEM_LIMIT),
        name="mlp",
    )(x, mod3, mod3, mod3, w1, w2, ln_g, ln_b)


def _rope_tables(seq_len):
    quarter = HEAD_DIM // 4
    inv = 1.0 / (ROPE_THETA ** (jnp.arange(quarter, dtype=F32) / quarter))
    t = jnp.arange(seq_len)
    ang_r = (t // GRID_W).astype(F32)[:, None] * inv[None, :]
    ang_c = (t % GRID_W).astype(F32)[:, None] * inv[None, :]
    zero = jnp.zeros_like(ang_r)
    cos = jnp.concatenate([jnp.cos(ang_r)] * 2 + [jnp.cos(ang_c)] * 2, axis=1)
    sdn = jnp.concatenate([-jnp.sin(ang_r), zero, -jnp.sin(ang_c), zero], axis=1)
    sup = jnp.concatenate([zero, jnp.sin(ang_r), zero, jnp.sin(ang_c)], axis=1)
    return tuple(jnp.tile(a, (1, LANES // HEAD_DIM)) for a in (cos, sdn, sup))


def kernel(x, c, ctx, c_ctx, w_ada, b_ada, w_in, w_attn_branch, w_pool, pool_scale, attn_sink,
           w_out, ln1_g, ln1_b, w_mlp_in, w_mlp_out, ln2_g, ln2_b):
    assert w_ada.shape[0] == 1, "single-layer block"
    B, S, _ = x.shape
    assert B + 1 <= MOD_ROWS
    cc = jnp.zeros((MOD_ROWS, D_MODEL), F32).at[:B].set(c).at[MOD_ROWS // 2].set(c_ctx)
    mod = _mod_call(cc, w_ada[0], b_ada[0][None, :])
    mod3 = mod.reshape(MOD_ROWS, 1, N_MOD * D_MODEL)

    w_in_b = w_in[0].astype(BF16)
    cos_t, sdn_t, sup_t = _rope_tables(S)
    q, kt, vd, p, g = _inproj_call(x, mod3, w_in_b, cos_t, sdn_t, sup_t, tm=512)
    kzc, vzc = _ctxproj_call(ctx, mod3, w_in_b[:, Q_WIDTH:Q_WIDTH + 2 * KV_WIDTH])

    x1 = _mix_call(attn_sink[0], x, q, kt, vd, p, g, kzc, vzc, mod3,
                   w_attn_branch[0].astype(BF16), w_pool[0].astype(BF16), pool_scale[0][None, :],
                   w_out[0].astype(BF16), ln1_g[0][None, :], ln1_b[0][None, :], tq=256)
    return _mlp_call(x1, mod3, w_mlp_in[0].astype(BF16), w_mlp_out[0].astype(BF16),
                     ln2_g[0][None, :], ln2_b[0][None, :], tm=512, ck=512)
```
